```python
import math
import jax
import jax.numpy as jnp
from jax import lax
import numpy as np

D_MODEL = 1024
BATCH = 16
SEQ = 256
DEPTH = 2
DEC_BATCH = 4
DEC_SEQ = 1024
PAST_LEN = 256

GRID_W = 64
HEAD_DIM = 64
D_MIX = D_MODEL
GROUP_W = D_MIX // 4
H_A = GROUP_W // HEAD_DIM
H_B = GROUP_W // HEAD_DIM
KV_B = H_B // 2
H_C = GROUP_W // HEAD_DIM
KV_C = H_C // 2
D_HY = GROUP_W
HY_ORDER = 2
HY_IN = (HY_ORDER + 1) * D_HY
SHORT_CONV = 3
FILTER_EMB = 33
FILTER_FW = 64
HY_FAST_DECAY = 0.3
HY_SLOW_DECAY = 1.5
HY_TARGET = 0.01
D_FF = 256 * ((8 * D_MODEL // 3 + 255) // 256)
N_MOD = 9
MLSTM_CHUNK = 64
Q_BLOCK = 128
WINDOW = 128
ROPE_BASE = 10000.0
EPS = 1e-6
IN_SIZES = (GROUP_W, GROUP_W, GROUP_W, GROUP_W, 2 * H_A, 2 * H_A,
            GROUP_W, KV_B * HEAD_DIM, KV_B * HEAD_DIM,
            GROUP_W, KV_C * HEAD_DIM, KV_C * HEAD_DIM,
            HY_IN)
N_IN = sum(IN_SIZES)
F32 = jnp.float32

kernel_name = 'hybrid_diffusion_trunk_step'


def rmsnorm(x, g):
    xf = x.astype(F32)
    y = xf * lax.rsqrt(jnp.mean(xf * xf, -1, keepdims=True) + EPS)
    return (y * g.astype(F32)).astype(x.dtype)


def swiglu(h, wg, wu, wd):
    return (jax.nn.silu(h @ wg) * (h @ wu)) @ wd


def split_in(u):
    idx = np.cumsum(IN_SIZES)[:-1].tolist()
    return jnp.split(u, idx, axis=-1)


def _heads(a, nh):
    return a.reshape(a.shape[0], a.shape[1], nh, HEAD_DIM)


def _sink(s):
    return s.astype(F32).reshape(KV_C, H_C // KV_C, 1, 1)


def grid_positions(n):
    rows = n // GRID_W
    r, c = jnp.meshgrid(jnp.arange(rows), jnp.arange(GRID_W), indexing='ij')
    return r.reshape(-1).astype(F32), c.reshape(-1).astype(F32)


def rope_1d(x, pos):
    half = x.shape[-1] // 2
    inv = ROPE_BASE ** (-jnp.arange(half, dtype=F32) / half)
    ang = pos[:, None] * inv[None, :]
    cos = jnp.cos(ang)[None, :, None, :]
    sin = jnp.sin(ang)[None, :, None, :]
    xf = x.astype(F32)
    x1, x2 = xf[..., :half], xf[..., half:]
    return jnp.concatenate([x1 * cos - x2 * sin, x2 * cos + x1 * sin], -1).astype(x.dtype)


def rope_2d(x, row, col):
    h = x.shape[-1] // 2
    return jnp.concatenate([rope_1d(x[..., :h], row), rope_1d(x[..., h:], col)], -1)


def attn_probs(s, sink):
    if sink is None:
        return jax.nn.softmax(s, axis=-1)
    m = jnp.maximum(jnp.max(s, -1, keepdims=True), sink)
    e = jnp.exp(s - m)
    return e / (jnp.sum(e, -1, keepdims=True) + jnp.exp(sink - m))


def attend_blocks(q, k, v, sink):
    b, n, h, d = q.shape
    kv = k.shape[2]
    g = h // kv
    nb = n // Q_BLOCK
    qb = (q * d ** -0.5).reshape(b, nb, Q_BLOCK, kv, g, d).transpose(1, 0, 2, 3, 4, 5)

    def one(qblk):
        s = jnp.einsum('bqkgd,bmkd->bkgqm', qblk, k).astype(F32)
        p = attn_probs(s, sink)
        return jnp.einsum('bkgqm,bmkd->bqkgd', p.astype(v.dtype), v)

    o = lax.map(one, qb)
    return o.transpose(1, 0, 2, 3, 4, 5).reshape(b, n, h * d)


def swa_banded(q, k, v, k_ctx, v_ctx, sink):
    b, n, h, d = q.shape
    kv = k.shape[2]
    g = h // kv
    nb = n // Q_BLOCK
    side = -(-WINDOW // Q_BLOCK)
    padn = side * Q_BLOCK

    def band(a):
        ap = jnp.pad(a, ((0, 0), (padn, padn), (0, 0), (0, 0))).reshape(b, nb + 2 * side, Q_BLOCK, kv, d)
        return jnp.concatenate([ap[:, j:j + nb] for j in range(2 * side + 1)], axis=2)

    kw, vw = band(k), band(v)
    mw = (2 * side + 1) * Q_BLOCK
    qpos = jnp.arange(nb)[:, None] * Q_BLOCK + jnp.arange(Q_BLOCK)[None, :]
    kpos = jnp.arange(nb)[:, None] * Q_BLOCK - padn + jnp.arange(mw)[None, :]
    kp = kpos[:, None, :]
    valid = (jnp.abs(kp - qpos[:, :, None]) <= WINDOW) & (kp >= 0) & (kp < n)
    qb = (q * d ** -0.5).reshape(b, nb, Q_BLOCK, kv, g, d)
    s_ctx = jnp.einsum('bnqkgd,bmkd->bnkgqm', qb, k_ctx).astype(F32)
    s_win = jnp.einsum('bnqkgd,bnmkd->bnkgqm', qb, kw).astype(F32)
    s_win = jnp.where(valid[None, :, None, None], s_win, -jnp.inf)
    p = attn_probs(jnp.concatenate([s_ctx, s_win], -1), sink)
    lc = k_ctx.shape[1]
    o = (jnp.einsum('bnkgqm,bmkd->bnqkgd', p[..., :lc].astype(v.dtype), v_ctx)
         + jnp.einsum('bnkgqm,bnmkd->bnqkgd', p[..., lc:].astype(v.dtype), vw))
    return o.reshape(b, n, h * d)


def mlstm_chunked(q, k, v, ig, fg, state):
    b, h, L, dk = q.shape
    nc = L // MLSTM_CHUNK
    k = k * (dk ** -0.5)
    logf = jax.nn.log_sigmoid(fg)

    def chunks(a):
        return jnp.moveaxis(a.reshape(b, h, nc, MLSTM_CHUNK, *a.shape[3:]), 2, 0)

    tri = jnp.arange(MLSTM_CHUNK)[:, None] >= jnp.arange(MLSTM_CHUNK)[None, :]

    def step(carry, inp):
        C, n, m = carry
        qc, kc, vc, ic, lf = inp
        bcum = jnp.cumsum(lf, -1)
        dmat = jnp.where(tri, bcum[..., :, None] - bcum[..., None, :] + ic[..., None, :], -jnp.inf)
        inter = bcum + m[..., None]
        mt = jnp.maximum(jnp.max(dmat, -1), inter)
        s = jnp.einsum('bhtd,bhsd->bhts', qc, kc) * jnp.exp(dmat - mt[..., None])
        w_inter = jnp.exp(inter - mt)
        num = jnp.einsum('bhts,bhse->bhte', s, vc) + w_inter[..., None] * jnp.einsum('bhtd,bhde->bhte', qc, C)
        den = jnp.sum(s, -1) + w_inter * jnp.einsum('bhtd,bhd->bht', qc, n)
        hc = num / jnp.maximum(jnp.abs(den), jnp.exp(-mt))[..., None]
        btot = bcum[..., -1]
        wlog = btot[..., None] - bcum + ic
        m_new = jnp.maximum(btot + m, jnp.max(wlog, -1))
        wk = jnp.exp(wlog - m_new[..., None])
        decay = jnp.exp(btot + m - m_new)
        C_new = decay[..., None, None] * C + jnp.einsum('bhs,bhsd,bhse->bhde', wk, kc, vc)
        n_new = decay[..., None] * n + jnp.einsum('bhs,bhsd->bhd', wk, kc)
        return (C_new, n_new, m_new), hc

    final, hs = lax.scan(step, state, (chunks(q), chunks(k), chunks(v), chunks(ig), chunks(logf)))
    hs = jnp.moveaxis(hs, 0, 2).reshape(b, h, L, v.shape[-1])
    return hs, final


def mlstm_bidir(q, k, v, o, ig, fg, state_f, state_b, p):
    b, L, _ = q.shape

    def heads(a):
        return a.astype(F32).reshape(b, L, H_A, HEAD_DIM).transpose(0, 2, 1, 3)

    gb = p['b_gates'].astype(F32)

    def gates(a, bias):
        return (a.astype(F32) + bias).reshape(b, L, 2, H_A).transpose(2, 0, 3, 1)

    igh = gates(ig, gb[:2 * H_A])
    fgh = gates(fg, gb[2 * H_A:])
    qh, kh, vh = heads(q), heads(k), heads(v)
    h_f, st_f = mlstm_chunked(qh, kh, vh, igh[0], fgh[0], state_f)

    def rev(a):
        return jnp.flip(a, axis=2)

    h_b, st_b = mlstm_chunked(rev(qh), rev(kh), rev(vh), rev(igh[1]), rev(fgh[1]), state_b)
    hs = (h_f + rev(h_b)).transpose(0, 2, 1, 3)
    hs = hs * lax.rsqrt(jnp.mean(hs * hs, -1, keepdims=True) + EPS)
    y = hs.reshape(b, L, GROUP_W) * p['g_mlstm'].astype(F32) * jax.nn.sigmoid(o.astype(F32))
    return y.astype(q.dtype), st_f, st_b


def hyena_filter(L, p):
    pos = jnp.arange(L, dtype=F32)
    bands = (FILTER_EMB - 1) // 2
    t01 = pos / max(L - 1, 1)
    ang = (2.0 * math.pi / L) * pos[:, None] * jnp.linspace(1e-4, bands - 1, bands, dtype=F32)[None, :]
    feats = jnp.concatenate([t01[:, None], jnp.cos(ang), -jnp.sin(ang)], -1)
    fr = p['filt_freq'].astype(F32)
    z = jnp.sin(fr * (feats @ p['filt_w1'].astype(F32) + p['filt_b1'].astype(F32)))
    z = jnp.sin(fr * (z @ p['filt_w2'].astype(F32) + p['filt_b2'].astype(F32)))
    filt = z @ p['filt_w3'].astype(F32) + p['filt_b3'].astype(F32)
    centre = L // 2
    dist = jnp.abs(pos - centre) / max(centre, 1)
    deltas = jnp.abs(jnp.linspace(math.log(HY_TARGET) / HY_SLOW_DECAY, math.log(HY_TARGET) / HY_FAST_DECAY, D_HY, dtype=F32))
    return filt * jnp.exp(-dist[:, None] * deltas[None, :])


def hyena_mixer(u, p):
    b, L, _ = u.shape
    half = SHORT_CONV // 2
    up = jnp.pad(u, ((0, 0), (half, half), (0, 0)))
    cw = p['conv_w']
    uc = sum(cw[j] * up[:, j:j + L] for j in range(SHORT_CONV)) + p['conv_b']
    x0, x1, v = jnp.split(uc.astype(F32), HY_ORDER + 1, axis=-1)
    z = x1 * v
    n_fft = 2 * L
    zf = jnp.fft.rfft(z, n=n_fft, axis=1)
    hf = jnp.fft.rfft(hyena_filter(L, p), n=n_fft, axis=0)
    yf = jnp.fft.irfft(zf * hf[None], n=n_fft, axis=1)
    y = yf[:, L // 2:L // 2 + L] + z * p['hyena_bias'].astype(F32)
    return (x0 * y).astype(u.dtype)


def mix_context(h, p):
    b = h.shape[0]
    aq, ak, av, ao, ai, af, bq, bk, bv, cq, ck, cv, du = split_in(h @ p['w_in'])
    zero = (jnp.zeros((b, H_A, HEAD_DIM, HEAD_DIM), F32), jnp.zeros((b, H_A, HEAD_DIM), F32), jnp.zeros((b, H_A), F32))
    ya, st_f, st_b = mlstm_bidir(aq, ak, av, ao, ai, af, zero, zero, p)
    kB = rmsnorm(_heads(bk, KV_B), p['g_knorm'])
    vB = _heads(bv, KV_B)
    yb = attend_blocks(rmsnorm(_heads(bq, H_B), p['g_qnorm']), kB, vB, None)
    kC = _heads(ck, KV_C)
    vC = _heads(cv, KV_C)
    yc = attend_blocks(_heads(cq, H_C), kC, vC, _sink(p['sinks']))
    yd = hyena_mixer(du, p)
    y = jnp.concatenate([ya, yb, yc, yd], -1) @ p['w_out']
    sC = jnp.stack([st_f[0], st_b[0]], axis=1)
    sn = jnp.stack([st_f[1], st_b[1]], axis=1)
    sm = jnp.stack([st_f[2], st_b[2]], axis=1)
    return y, (sC, sn, sm, kB, vB, kC, vC)


def mix_latent(h, cache, p):
    b, n, _ = h.shape
    row, col = grid_positions(n)
    sC, sn, sm, gk, gv, sk, sv = cache
    aq, ak, av, ao, ai, af, bq, bk, bv, cq, ck, cv, du = split_in(h @ p['w_in'])
    st_f = (sC[:, 0].astype(F32), sn[:, 0].astype(F32), sm[:, 0].astype(F32))
    st_b = (sC[:, 1].astype(F32), sn[:, 1].astype(F32), sm[:, 1].astype(F32))
    ya, _, _ = mlstm_bidir(aq, ak, av, ao, ai, af, st_f, st_b, p)
    qB = rope_2d(rmsnorm(_heads(bq, H_B), p['g_qnorm']), row, col)
    kB = rope_2d(rmsnorm(_heads(bk, KV_B), p['g_knorm']), row, col)
    vB = _heads(bv, KV_B)
    yb = attend_blocks(qB, jnp.concatenate([gk.astype(kB.dtype), kB], 1),
                       jnp.concatenate([gv.astype(vB.dtype), vB], 1), None)
    qC = rope_2d(_heads(cq, H_C), row, col)
    kC = rope_2d(_heads(ck, KV_C), row, col)
    vC = _heads(cv, KV_C)
    yc = swa_banded(qC, kC, vC, sk.astype(kC.dtype), sv.astype(vC.dtype), _sink(p['sinks']))
    yd = hyena_mixer(du, p)
    return jnp.concatenate([ya, yb, yc, yd], -1) @ p['w_out'], None


def trunk_layer(x, mod, p, mixer):
    def part(i):
        return mod[:, i][:, None, :]

    h = rmsnorm(x, p['g_ff1']) * (1 + part(1)) + part(0)
    x = x + 0.5 * part(2) * swiglu(h, p['w1_gate'], p['w1_up'], p['w1_down'])
    h = rmsnorm(x, p['g_mix']) * (1 + part(4)) + part(3)
    y, ctx_state = mixer(h)
    x = x + part(5) * y
    h = rmsnorm(x, p['g_ff2']) * (1 + part(7)) + part(6)
    x = x + 0.5 * part(8) * swiglu(h, p['w2_gate'], p['w2_up'], p['w2_down'])
    return x, ctx_state


def setup_inputs(seed: int = 0) -> dict:
    key = jax.random.key(seed)
    keys = list(jax.random.split(key, 64))

    def nrm(shape, scale=1.0):
        return jax.random.normal(keys.pop(), shape, F32) * scale

    def gain(shape):
        return 1.0 + nrm(shape, 0.02)

    L = DEPTH
    i_bias = nrm((L, 2 * H_A), 0.1)
    f_bias = jnp.tile(jnp.linspace(3.0, 6.0, H_A), 2)[None, :] + nrm((L, 2 * H_A), 0.1)
    return {
        'x_prompt': nrm((BATCH, SEQ, D_MODEL)),
        'x_sample': nrm((DEC_BATCH, DEC_SEQ, D_MODEL)),
        'state_mlstm_C': nrm((DEC_BATCH, DEPTH, 2, H_A, HEAD_DIM, HEAD_DIM), 0.3),
        'state_mlstm_n': nrm((DEC_BATCH, DEPTH, 2, H_A, HEAD_DIM), 0.3),
        'state_mlstm_m': nrm((DEC_BATCH, DEPTH, 2, H_A)),
        'cache_gattn_k': nrm((DEC_BATCH, DEPTH, PAST_LEN, KV_B, HEAD_DIM)),
        'cache_gattn_v': nrm((DEC_BATCH, DEPTH, PAST_LEN, KV_B, HEAD_DIM)),
        'cache_swa_k': nrm((DEC_BATCH, DEPTH, PAST_LEN, KV_C, HEAD_DIM)),
        'cache_swa_v': nrm((DEC_BATCH, DEPTH, PAST_LEN, KV_C, HEAD_DIM)),
        'c': nrm((DEC_BATCH, D_MODEL)),
        'c_ctx': nrm((D_MODEL,)),
        'w_ada': nrm((L, D_MODEL, N_MOD * D_MODEL), 0.5 * D_MODEL ** -0.5),
        'b_ada': nrm((L, N_MOD * D_MODEL), 0.01),
        'g_ff1': gain((L, D_MODEL)),
        'w1_gate': nrm((L, D_MODEL, D_FF), D_MODEL ** -0.5),
        'w1_up': nrm((L, D_MODEL, D_FF), D_MODEL ** -0.5),
        'w1_down': nrm((L, D_FF, D_MODEL), D_FF ** -0.5),
        'g_mix': gain((L, D_MODEL)),
        'w_in': nrm((L, D_MODEL, N_IN), D_MODEL ** -0.5),
        'b_gates': jnp.concatenate([i_bias, f_bias], -1),
        'g_mlstm': gain((L, GROUP_W)),
        'g_qnorm': gain((L, HEAD_DIM)),
        'g_knorm': gain((L, HEAD_DIM)),
        'sinks': nrm((L, H_C), 0.5),
        'conv_w': nrm((L, SHORT_CONV, HY_IN), 0.5),
        'conv_b': nrm((L, HY_IN), 0.02),
        'filt_w1': nrm((L, FILTER_EMB, FILTER_FW), FILTER_EMB ** -0.5),
        'filt_b1': nrm((L, FILTER_FW), 0.1),
        'filt_w2': nrm((L, FILTER_FW, FILTER_FW), FILTER_FW ** -0.5),
        'filt_b2': nrm((L, FILTER_FW), 0.1),
        'filt_w3': nrm((L, FILTER_FW, D_HY), 0.05 * FILTER_FW ** -0.5),
        'filt_b3': nrm((L, D_HY), 0.01),
        'filt_freq': 1.0 + nrm((L, FILTER_FW), 0.1),
        'hyena_bias': nrm((L, D_HY), 0.1),
        'w_out': nrm((L, D_MIX, D_MODEL), D_MIX ** -0.5),
        'g_ff2': gain((L, D_MODEL)),
        'w2_gate': nrm((L, D_MODEL, D_FF), D_MODEL ** -0.5),
        'w2_up': nrm((L, D_MODEL, D_FF), D_MODEL ** -0.5),
        'w2_down': nrm((L, D_FF, D_MODEL), D_FF ** -0.5),
        'g_final': gain((D_MODEL,)),
    }


def reference(x_prompt, x_sample, state_mlstm_C, state_mlstm_n, state_mlstm_m,
              cache_gattn_k, cache_gattn_v, cache_swa_k, cache_swa_v, c, c_ctx,
              w_ada, b_ada, g_ff1, w1_gate, w1_up, w1_down, g_mix, w_in, b_gates,
              g_mlstm, g_qnorm, g_knorm, sinks, conv_w, conv_b, filt_w1, filt_b1,
              filt_w2, filt_b2, filt_w3, filt_b3, filt_freq, hyena_bias, w_out,
              g_ff2, w2_gate, w2_up, w2_down, g_final):
    new_C, new_n, new_m, new_gk, new_gv, new_sk, new_sv = [], [], [], [], [], [], []
    xp, xs = x_prompt, x_sample
    for l in range(DEPTH):
        p = {
            'g_ff1': g_ff1[l], 'w1_gate': w1_gate[l], 'w1_up': w1_up[l], 'w1_down': w1_down[l],
            'g_mix': g_mix[l], 'w_in': w_in[l], 'b_gates': b_gates[l], 'g_mlstm': g_mlstm[l],
            'g_qnorm': g_qnorm[l], 'g_knorm': g_knorm[l], 'sinks': sinks[l],
            'conv_w': conv_w[l], 'conv_b': conv_b[l],
            'filt_w1': filt_w1[l], 'filt_b1': filt_b1[l], 'filt_w2': filt_w2[l], 'filt_b2': filt_b2[l],
            'filt_w3': filt_w3[l], 'filt_b3': filt_b3[l], 'filt_freq': filt_freq[l],
            'hyena_bias': hyena_bias[l], 'w_out': w_out[l],
            'g_ff2': g_ff2[l], 'w2_gate': w2_gate[l], 'w2_up': w2_up[l], 'w2_down': w2_down[l],
        }
        mod_ctx = (jax.nn.silu(c_ctx) @ w_ada[l] + b_ada[l]).reshape(1, N_MOD, -1)
        mod_lat = (jax.nn.silu(c) @ w_ada[l] + b_ada[l]).reshape(c.shape[0], N_MOD, -1)
        xp, st = trunk_layer(xp, mod_ctx, p, lambda h: mix_context(h, p))
        new_C.append(st[0]); new_n.append(st[1]); new_m.append(st[2])
        new_gk.append(st[3]); new_gv.append(st[4]); new_sk.append(st[5]); new_sv.append(st[6])
        cache_l = (state_mlstm_C[:, l], state_mlstm_n[:, l], state_mlstm_m[:, l],
                   cache_gattn_k[:, l], cache_gattn_v[:, l], cache_swa_k[:, l], cache_swa_v[:, l])
        xs, _ = trunk_layer(xs, mod_lat, p, lambda h: mix_latent(h, cache_l, p))
    y_prompt = rmsnorm(xp, g_final)
    y_sample = rmsnorm(xs, g_final)
    return (y_prompt, y_sample,
            jnp.stack(new_C, axis=1), jnp.stack(new_n, axis=1), jnp.stack(new_m, axis=1),
            jnp.stack(new_gk, axis=1), jnp.stack(new_gv, axis=1),
            jnp.stack(new_sk, axis=1), jnp.stack(new_sv, axis=1))
```

```python
import functools
import math

import jax
import jax.numpy as jnp
import numpy as np
from jax import lax
from jax.experimental import pallas as pl
from jax.experimental.pallas import tpu as pltpu

D_MODEL = 1024
DEPTH = 2
HEAD_DIM = 64
GRID_W = 64
GROUP_W = D_MODEL // 4
H_A = GROUP_W // HEAD_DIM
N_GATES = 4 * H_A
KV_W = GROUP_W // 2
HY_IN = 3 * GROUP_W
FILTER_EMB = 33
FILTER_FW = 64
HY_FAST_DECAY = 0.3
HY_SLOW_DECAY = 1.5
HY_TARGET = 0.01
D_FF = 256 * ((8 * D_MODEL // 3 + 255) // 256)
N_MOD = 9
WINDOW = 128
ROPE_BASE = 10000.0
EPS = 1e-6

LANES = 128
MLSTM_T = 256
TOKEN_TILE = 512
FF_TILE = 256
ADA_TILE = 1024
VMEM_LIMIT = 56 * 1024 * 1024

F32 = jnp.float32
BF16 = jnp.bfloat16
NEG_INF = float("-inf")


def _params(*sem):
    return pltpu.CompilerParams(dimension_semantics=sem, vmem_limit_bytes=VMEM_LIMIT)


def _bdot(a, b):
    return jnp.dot(a.astype(BF16), b.astype(BF16), preferred_element_type=F32)


def _bdot_nt(a, b):
    return lax.dot_general(a.astype(BF16), b.astype(BF16), (((1,), (1,)), ((), ())),
                           preferred_element_type=F32)


def _bdot_tn(a, b):
    return lax.dot_general(a.astype(BF16), b.astype(BF16), (((0,), (0,)), ((), ())),
                           preferred_element_type=F32)


def _hdot(a, b):
    return jnp.dot(a, b, precision=lax.Precision.HIGHEST, preferred_element_type=F32)


def _rms(x, g):
    return x * lax.rsqrt(jnp.mean(x * x, axis=-1, keepdims=True) + EPS) * g


def _modulated_norm(x, g, mod_ref, base):
    return _rms(x, g) * (1.0 + mod_ref[base + 1:base + 2, :]) + mod_ref[base:base + 1, :]


def _ada_kernel(c_ref, w_ref, b_ref, o_ref):
    c = c_ref[...]
    o_ref[...] = _bdot(c * jax.nn.sigmoid(c), w_ref[...]) + b_ref[...]


def _ada(cc, w_ada, b_ada):
    n = N_MOD * D_MODEL
    return pl.pallas_call(
        _ada_kernel,
        grid=(DEPTH, n // ADA_TILE),
        in_specs=[pl.BlockSpec((8, D_MODEL), lambda l, j: (0, 0)),
                  pl.BlockSpec((None, D_MODEL, ADA_TILE), lambda l, j: (l, 0, j)),
                  pl.BlockSpec((None, 1, ADA_TILE), lambda l, j: (l, 0, j))],
        out_specs=pl.BlockSpec((None, 8, ADA_TILE), lambda l, j: (l, 0, j)),
        out_shape=jax.ShapeDtypeStruct((DEPTH, 8, n), F32),
        compiler_params=_params("arbitrary", "arbitrary"),
        name="ada",
    )(cc, w_ada, b_ada.reshape(DEPTH, 1, n))


def _ffn_kernel(x_ref, mod_ref, g_ref, wg_ref, wu_ref, wd_ref, *rest, base, final):
    if final:
        gf_ref, o_ref, h_sc, acc_sc = rest
    else:
        o_ref, h_sc, acc_sc = rest
    f = pl.program_id(1)

    @pl.when(f == 0)
    def _():
        h_sc[...] = _modulated_norm(x_ref[...], g_ref[...], mod_ref, base).astype(BF16)
        acc_sc[...] = jnp.zeros_like(acc_sc)

    h = h_sc[...]
    a = jnp.dot(h, wg_ref[...], preferred_element_type=F32)
    b = jnp.dot(h, wu_ref[...], preferred_element_type=F32)
    z = (a * jax.nn.sigmoid(a)) * b
    acc_sc[...] += jnp.dot(z.astype(BF16), wd_ref[...], preferred_element_type=F32)

    @pl.when(f == pl.num_programs(1) - 1)
    def _():
        y = x_ref[...] + 0.5 * mod_ref[base + 2:base + 3, :] * acc_sc[...]
        o_ref[...] = _rms(y, gf_ref[...]) if final else y


def _ffn(x, mod, g, wg, wu, wd, base, g_final=None):
    n = x.shape[0]
    tiles_per_mod = n // mod.shape[0] // TOKEN_TILE
    final = g_final is not None
    in_specs = [pl.BlockSpec((TOKEN_TILE, D_MODEL), lambda i, f: (i, 0)),
                pl.BlockSpec((None, N_MOD, D_MODEL), lambda i, f: (i // tiles_per_mod, 0, 0)),
                pl.BlockSpec((1, D_MODEL), lambda i, f: (0, 0)),
                pl.BlockSpec((D_MODEL, FF_TILE), lambda i, f: (0, f)),
                pl.BlockSpec((D_MODEL, FF_TILE), lambda i, f: (0, f)),
                pl.BlockSpec((FF_TILE, D_MODEL), lambda i, f: (f, 0))]
    args = [x, mod, g, wg, wu, wd]
    if final:
        in_specs.append(pl.BlockSpec((1, D_MODEL), lambda i, f: (0, 0)))
        args.append(g_final)
    return pl.pallas_call(
        functools.partial(_ffn_kernel, base=base, final=final),
        grid=(n // TOKEN_TILE, D_FF // FF_TILE),
        in_specs=in_specs,
        out_specs=pl.BlockSpec((TOKEN_TILE, D_MODEL), lambda i, f: (i, 0)),
        out_shape=jax.ShapeDtypeStruct((n, D_MODEL), F32),
        scratch_shapes=[pltpu.VMEM((TOKEN_TILE, D_MODEL), BF16),
                        pltpu.VMEM((TOKEN_TILE, D_MODEL), F32)],
        compiler_params=_params("arbitrary", "arbitrary"),
        name="ffn",
    )(*args)


def _inproj_kernel(x_ref, mod_ref, g_ref, wa_ref, wgc_ref, wgr_ref, wb_ref, wc_ref, wd_ref,
                   ua_ref, gc_ref, gr_ref, ub_ref, uc_ref, ud_ref):
    h = _modulated_norm(x_ref[...], g_ref[...], mod_ref, 3).astype(BF16)
    ua_ref[...] = jnp.dot(h, wa_ref[...], preferred_element_type=F32)
    gc_ref[...] = jnp.dot(h, wgc_ref[...], preferred_element_type=F32)
    gr_ref[...] = _bdot_nt(wgr_ref[...], h)
    ub_ref[...] = jnp.dot(h, wb_ref[...], preferred_element_type=F32)
    uc_ref[...] = jnp.dot(h, wc_ref[...], preferred_element_type=F32)
    ud_ref[...] = jnp.dot(h, wd_ref[...], preferred_element_type=F32)


def _inproj(x, mod, g, w):
    n = x.shape[0]
    tiles_per_mod = n // mod.shape[0] // TOKEN_TILE

    def resident(a):
        return pl.BlockSpec(a.shape, lambda i: (0, 0))

    def rows(width):
        return pl.BlockSpec((TOKEN_TILE, width), lambda i: (i, 0))

    widths = (D_MODEL, LANES, None, 2 * GROUP_W, 2 * GROUP_W, HY_IN)
    out_specs = [pl.BlockSpec((N_GATES, TOKEN_TILE), lambda i: (0, i)) if wd is None else rows(wd)
                 for wd in widths]
    out_shape = [jax.ShapeDtypeStruct((N_GATES, n) if wd is None else (n, wd), F32) for wd in widths]
    return pl.pallas_call(
        _inproj_kernel,
        grid=(n // TOKEN_TILE,),
        in_specs=[rows(D_MODEL),
                  pl.BlockSpec((None, N_MOD, D_MODEL), lambda i: (i // tiles_per_mod, 0, 0)),
                  resident(g)] + [resident(a) for a in w],
        out_specs=out_specs,
        out_shape=out_shape,
        compiler_params=_params("arbitrary"),
        name="inproj",
    )(x, mod, g, *w)


def _log_sigmoid(x):
    return jnp.minimum(x, 0.0) - jnp.log1p(jnp.exp(-jnp.abs(x)))


def _mlstm_kernel(u_ref, gc_ref, gr_ref, bgc_ref, bgr_ref, gm_ref, c0_ref, m0_ref,
                  y_ref, ct_ref, mt_ref, hs_sc, *, seq):
    t = MLSTM_T
    nc = seq // t
    ri = lax.broadcasted_iota(jnp.int32, (t, t), 0)
    ci = lax.broadcasted_iota(jnp.int32, (t, t), 1)
    low = ci <= ri
    upp = ci >= ri
    lowf = low.astype(F32)
    uppf = upp.astype(F32)

    gr = gr_ref[...] + bgr_ref[...]
    gc = gc_ref[...] + bgc_ref[...]
    lf_r = _log_sigmoid(gr[2 * H_A:, :])
    lf_c = _log_sigmoid(gc)
    states = [c0_ref[k] for k in range(2 * H_A)]
    ms = [m0_ref[k:k + 1, 0:1] for k in range(2 * H_A)]

    def instance(direction, head, c):
        k = direction * H_A + head
        cs = slice(c * t, (c + 1) * t)
        cum_r = _hdot(lf_r[:, cs], uppf if direction == 0 else lowf)
        cum_c = _hdot(lowf if direction == 0 else uppf, lf_c[cs, :])
        br = cum_r[k:k + 1, :]
        bc = cum_c[:, 2 * H_A + k:2 * H_A + k + 1]
        i_r = gr[k:k + 1, cs]
        i_c = gc[cs, k:k + 1]
        m = ms[k]
        dm = jnp.where(low if direction == 0 else upp, bc - br + i_r, NEG_INF)
        inter = bc + m
        mt = jnp.maximum(jnp.max(dm, axis=-1, keepdims=True), inter)
        q = u_ref[cs, head * HEAD_DIM:(head + 1) * HEAD_DIM]
        kk = u_ref[cs, GROUP_W + head * HEAD_DIM:GROUP_W + (head + 1) * HEAD_DIM] * (HEAD_DIM ** -0.5)
        v = u_ref[cs, 2 * GROUP_W + head * HEAD_DIM:2 * GROUP_W + (head + 1) * HEAD_DIM]
        v_aug = jnp.concatenate([v, jnp.ones_like(v)], axis=1).astype(BF16)
        s = _bdot_nt(q, kk) * jnp.exp(dm - mt)
        w_inter = jnp.exp(inter - mt)
        na = _bdot(s, v_aug) + w_inter * _bdot(q, states[k])
        num = na[:, :HEAD_DIM]
        den = na[:, HEAD_DIM:HEAD_DIM + 1]
        hc = num / jnp.maximum(jnp.abs(den), jnp.exp(-mt))
        btot = bc[t - 1:t, :] if direction == 0 else bc[0:1, :]
        wlog = btot - bc + i_c
        m_new = jnp.maximum(btot + m, jnp.max(wlog, axis=0, keepdims=True))
        wk = jnp.exp(wlog - m_new)
        decay = jnp.exp(btot + m - m_new)
        states[k] = decay * states[k] + _bdot_tn(kk * wk, v_aug)
        ms[k] = m_new
        return hc

    for c in range(nc):
        hs_sc[c * t:(c + 1) * t, :] = jnp.concatenate(
            [instance(0, head, c) for head in range(H_A)], axis=1)
    for c in reversed(range(nc)):
        cs = slice(c * t, (c + 1) * t)
        outs = []
        for head in range(H_A):
            hsum = hs_sc[cs, head * HEAD_DIM:(head + 1) * HEAD_DIM] + instance(1, head, c)
            hn = hsum * lax.rsqrt(jnp.mean(hsum * hsum, axis=-1, keepdims=True) + EPS)
            o = u_ref[cs, 3 * GROUP_W + head * HEAD_DIM:3 * GROUP_W + (head + 1) * HEAD_DIM]
            outs.append(hn * gm_ref[:, head * HEAD_DIM:(head + 1) * HEAD_DIM] * jax.nn.sigmoid(o))
        y_ref[cs, :] = jnp.concatenate(outs, axis=1).astype(BF16)
    for k in range(2 * H_A):
        ct_ref[k] = states[k]
        mt_ref[k:k + 1, :] = jnp.broadcast_to(ms[k], (1, LANES))


def _mlstm(ua, gc, gr, bgc, bgr, gm, c0, m0, batch, seq):
    n = batch * seq
    return pl.pallas_call(
        functools.partial(_mlstm_kernel, seq=seq),
        grid=(batch,),
        in_specs=[pl.BlockSpec((seq, D_MODEL), lambda b: (b, 0)),
                  pl.BlockSpec((seq, LANES), lambda b: (b, 0)),
                  pl.BlockSpec((N_GATES, seq), lambda b: (0, b)),
                  pl.BlockSpec((1, LANES), lambda b: (0, 0)),
                  pl.BlockSpec((N_GATES, 1), lambda b: (0, 0)),
                  pl.BlockSpec((1, GROUP_W), lambda b: (0, 0)),
                  pl.BlockSpec((None, 2 * H_A, HEAD_DIM, LANES), lambda b: (b, 0, 0, 0)),
                  pl.BlockSpec((None, 2 * H_A, LANES), lambda b: (b, 0, 0))],
        out_specs=[pl.BlockSpec((seq, GROUP_W), lambda b: (b, 0)),
                   pl.BlockSpec((None, 2 * H_A, HEAD_DIM, LANES), lambda b: (b, 0, 0, 0)),
                   pl.BlockSpec((None, 2 * H_A, LANES), lambda b: (b, 0, 0))],
        out_shape=[jax.ShapeDtypeStruct((n, GROUP_W), BF16),
                   jax.ShapeDtypeStruct((batch, 2 * H_A, HEAD_DIM, LANES), F32),
                   jax.ShapeDtypeStruct((batch, 2 * H_A, LANES), F32)],
        scratch_shapes=[pltpu.VMEM((seq, GROUP_W), F32)],
        compiler_params=_params("arbitrary"),
        name="mlstm",
    )(ua, gc, gr, bgc, bgr, gm, c0, m0)


def _head_norm(x, g):
    parts = [_rms(x[:, i:i + HEAD_DIM], g) for i in range(0, x.shape[1], HEAD_DIM)]
    return jnp.concatenate(parts, axis=1)


def _rope(x, cos, sin):
    w = x.shape[1]
    lane = lax.broadcasted_iota(jnp.int32, x.shape, 1)
    partner = jnp.where(lane % 32 < 16, pltpu.roll(x, w - 16, axis=1), pltpu.roll(x, 16, axis=1))
    return x * cos + partner * sin


def _attn_kernel(*refs, seq, past, q_tile, norm, rope, sink, window, emit_k):
    refs = list(refs)
    q_ref, k_ref, v_ref = refs[:3]
    del refs[:3]
    if norm:
        gq_ref, gk_ref = refs[:2]
        del refs[:2]
    if rope:
        cos_ref, sin_ref = refs[:2]
        del refs[:2]
    if sink:
        sink_ref = refs.pop(0)
    if past:
        ck_ref, cv_ref = refs[:2]
        del refs[:2]
    y_ref = refs.pop(0)
    if emit_k:
        knew_ref = refs.pop(0)

    q = q_ref[...]
    k = k_ref[...]
    if norm:
        q = _head_norm(q, gq_ref[...])
        k = _head_norm(k, gk_ref[...])
    if emit_k:
        knew_ref[...] = k
    if rope:
        q = _rope(q, cos_ref[...], sin_ref[...])
        k = _rope(k, cos_ref[:, :KV_W], sin_ref[:, :KV_W])
    qs = (q * (HEAD_DIM ** -0.5)).astype(BF16)
    kb = k.astype(BF16)
    vb = v_ref[...].astype(BF16)
    if past:
        ckb = ck_ref[...].astype(BF16)
        cvb = cv_ref[...].astype(BF16)

    tq = q_tile
    for j in range(2):
        hs = slice(j * HEAD_DIM, (j + 1) * HEAD_DIM)
        kj, vj = kb[:, hs], vb[:, hs]
        for blk in range(seq // tq):
            rows = slice(blk * tq, (blk + 1) * tq)
            q2 = jnp.concatenate([qs[rows, 2 * j * HEAD_DIM:(2 * j + 1) * HEAD_DIM],
                                  qs[rows, (2 * j + 1) * HEAD_DIM:(2 * j + 2) * HEAD_DIM]], axis=0)
            parts = []
            if past:
                parts.append((_bdot_nt(q2, ckb[:, hs]), cvb[:, hs]))
            if window is None:
                parts.append((_bdot_nt(q2, kj), vj))
            else:
                span = tq + 2 * window
                st = min(max(blk * tq - window, 0), seq - span)
                s = _bdot_nt(q2, kj[st:st + span])
                qpos = blk * tq + lax.broadcasted_iota(jnp.int32, (2 * tq, span), 0) % tq
                kpos = st + lax.broadcasted_iota(jnp.int32, (2 * tq, span), 1)
                parts.append((jnp.where(jnp.abs(qpos - kpos) <= window, s, NEG_INF), vj[st:st + span]))
            m = functools.reduce(jnp.maximum, [jnp.max(s, axis=-1, keepdims=True) for s, _ in parts])
            if sink:
                first = lax.broadcasted_iota(jnp.int32, (2 * tq, 1), 0) < tq
                snk = jnp.where(first, sink_ref[0:1, 2 * j:2 * j + 1], sink_ref[0:1, 2 * j + 1:2 * j + 2])
                m = jnp.maximum(m, snk)
            es = [jnp.exp(s - m) for s, _ in parts]
            den = functools.reduce(jnp.add, [jnp.sum(e, axis=-1, keepdims=True) for e in es])
            if sink:
                den = den + jnp.exp(snk - m)
            o = functools.reduce(jnp.add, [_bdot(e, v) for e, (_, v) in zip(es, parts)]) / den
            y_ref[rows, 2 * j * HEAD_DIM:(2 * j + 2) * HEAD_DIM] = jnp.concatenate(
                [o[:tq], o[tq:]], axis=1).astype(BF16)


def _attn(u, batch, seq, *, q_tile, gq=None, gk=None, rope=None, sink=None, cache=None,
          window=None, emit_k=False):
    n = batch * seq
    past = 0 if cache is None else cache[0].shape[1]
    in_specs = [pl.BlockSpec((seq, GROUP_W), lambda b: (b, 0)),
                pl.BlockSpec((seq, KV_W), lambda b: (b, 2)),
                pl.BlockSpec((seq, KV_W), lambda b: (b, 3))]
    args = [u, u, u]
    if gq is not None:
        in_specs += [pl.BlockSpec((1, HEAD_DIM), lambda b: (0, 0))] * 2
        args += [gq, gk]
    if rope is not None:
        in_specs += [pl.BlockSpec((seq, GROUP_W), lambda b: (0, 0))] * 2
        args += list(rope)
    if sink is not None:
        in_specs.append(pl.BlockSpec((1, LANES), lambda b: (0, 0)))
        args.append(sink)
    if cache is not None:
        in_specs += [pl.BlockSpec((None, past, KV_W), lambda b: (b, 0, 0))] * 2
        args += list(cache)
    out_specs = [pl.BlockSpec((seq, GROUP_W), lambda b: (b, 0))]
    out_shape = [jax.ShapeDtypeStruct((n, GROUP_W), BF16)]
    if emit_k:
        out_specs.append(pl.BlockSpec((seq, KV_W), lambda b: (b, 0)))
        out_shape.append(jax.ShapeDtypeStruct((n, KV_W), F32))
    return pl.pallas_call(
        functools.partial(_attn_kernel, seq=seq, past=past, q_tile=q_tile, norm=gq is not None,
                          rope=rope is not None, sink=sink is not None, window=window, emit_k=emit_k),
        grid=(batch,),
        in_specs=in_specs,
        out_specs=out_specs,
        out_shape=out_shape,
        compiler_params=_params("arbitrary"),
        name="attn",
    )(*args)


def _hyena_filter_kernel(feat_ref, w1_ref, b1_ref, w2_ref, b2_ref, w3_ref, b3_ref, fr_ref,
                         win_ref, cf_ref, sf_ref, hre_ref, him_ref):
    fr = fr_ref[...]
    z = jnp.sin(fr * (_hdot(feat_ref[...], w1_ref[...]) + b1_ref[...]))
    z = jnp.sin(fr * (_hdot(z, w2_ref[...]) + b2_ref[...]))
    h = (_hdot(z, w3_ref[...]) + b3_ref[...]) * win_ref[...]
    hre_ref[...] = _hdot(cf_ref[...], h)
    him_ref[...] = _hdot(sf_ref[...], h)


def _hyena_filter(consts, w1, b1, w2, b2, w3, b3, fr):
    seq = consts["feats"].shape[0]
    return pl.pallas_call(
        _hyena_filter_kernel,
        out_shape=[jax.ShapeDtypeStruct((seq, GROUP_W), F32)] * 2,
        compiler_params=pltpu.CompilerParams(vmem_limit_bytes=VMEM_LIMIT),
        name="hyena_filter",
    )(consts["feats"], w1, b1, w2, b2, w3, b3, fr, consts["win"], consts["cf"], consts["sf"])


def _hyena_kernel(u_ref, cw_ref, cb_ref, bias_ref, hre_ref, him_ref, cf_ref, sf_ref, ci_ref, si_ref,
                  y_ref, *, seq):
    u = u_ref[...]
    row = lax.broadcasted_iota(jnp.int32, u.shape, 0)
    prev = jnp.where(row == 0, 0.0, pltpu.roll(u, 1, axis=0))
    nxt = jnp.where(row == seq - 1, 0.0, pltpu.roll(u, seq - 1, axis=0))
    uc = cw_ref[0:1, :] * prev + cw_ref[1:2, :] * u + cw_ref[2:3, :] * nxt + cb_ref[...]
    x0 = uc[:, :GROUP_W]
    z = uc[:, GROUP_W:2 * GROUP_W] * uc[:, 2 * GROUP_W:]
    zb = z.astype(BF16)
    zre = jnp.dot(cf_ref[...], zb, preferred_element_type=F32)
    zim = jnp.dot(sf_ref[...], zb, preferred_element_type=F32)
    hre, him = hre_ref[...], him_ref[...]
    yre = (zre * hre - zim * him).astype(BF16)
    yim = (zre * him + zim * hre).astype(BF16)
    y = (jnp.dot(ci_ref[...], yre, preferred_element_type=F32)
         + jnp.dot(si_ref[...], yim, preferred_element_type=F32))
    y_ref[...] = (x0 * (y + z * bias_ref[...])).astype(BF16)


def _to_bf16_kernel(*refs):
    for src, dst in zip(refs[:len(refs) // 2], refs[len(refs) // 2:]):
        dst[...] = src[...].astype(BF16)


def _dft_tables_bf16(consts):
    mats = [consts[name] for name in ("cf", "sf", "ci", "si")]
    seq = mats[0].shape[0]
    tile = min(seq, 256)
    spec = pl.BlockSpec((tile, seq), lambda i: (i, 0))
    return pl.pallas_call(
        _to_bf16_kernel,
        grid=(seq // tile,),
        in_specs=[spec] * 4,
        out_specs=[spec] * 4,
        out_shape=[jax.ShapeDtypeStruct((seq, seq), BF16)] * 4,
        compiler_params=_params("arbitrary"),
        name="dft_tables",
    )(*mats)


def _hyena(ud, cw, cb, bias, hre, him, mats, batch, seq):
    def resident(a):
        return pl.BlockSpec(a.shape, lambda b: (0, 0))

    return pl.pallas_call(
        functools.partial(_hyena_kernel, seq=seq),
        grid=(batch,),
        in_specs=[pl.BlockSpec((seq, HY_IN), lambda b: (b, 0)),
                  resident(cw), resident(cb), resident(bias), resident(hre), resident(him)]
                 + [resident(a) for a in mats],
        out_specs=pl.BlockSpec((seq, GROUP_W), lambda b: (b, 0)),
        out_shape=jax.ShapeDtypeStruct((batch * seq, GROUP_W), BF16),
        compiler_params=_params("arbitrary"),
        name="hyena",
    )(ud, cw, cb, bias, hre, him, *mats)


def _outproj_kernel(x_ref, mod_ref, ya_ref, yb_ref, yc_ref, yd_ref, w_ref, o_ref):
    y = jnp.dot(ya_ref[...], w_ref[0:GROUP_W, :], preferred_element_type=F32)
    y += jnp.dot(yb_ref[...], w_ref[GROUP_W:2 * GROUP_W, :], preferred_element_type=F32)
    y += jnp.dot(yc_ref[...], w_ref[2 * GROUP_W:3 * GROUP_W, :], preferred_element_type=F32)
    y += jnp.dot(yd_ref[...], w_ref[3 * GROUP_W:, :], preferred_element_type=F32)
    o_ref[...] = x_ref[...] + mod_ref[5:6, :] * y


def _outproj(x, mod, ys, w):
    n = x.shape[0]
    tiles_per_mod = n // mod.shape[0] // TOKEN_TILE
    return pl.pallas_call(
        _outproj_kernel,
        grid=(n // TOKEN_TILE,),
        in_specs=[pl.BlockSpec((TOKEN_TILE, D_MODEL), lambda i: (i, 0)),
                  pl.BlockSpec((None, N_MOD, D_MODEL), lambda i: (i // tiles_per_mod, 0, 0))]
                 + [pl.BlockSpec((TOKEN_TILE, GROUP_W), lambda i: (i, 0))] * 4
                 + [pl.BlockSpec((D_MODEL, D_MODEL), lambda i: (0, 0))],
        out_specs=pl.BlockSpec((TOKEN_TILE, D_MODEL), lambda i: (i, 0)),
        out_shape=jax.ShapeDtypeStruct((n, D_MODEL), F32),
        compiler_params=_params("arbitrary"),
        name="outproj",
    )(x, mod, *ys, w)


@functools.lru_cache(maxsize=None)
def _rope_tables(n):
    pos = np.arange(n)
    inv = ROPE_BASE ** (-np.arange(16, dtype=np.float64) / 16)
    ang_r = (pos // GRID_W)[:, None] * inv[None, :]
    ang_c = (pos % GRID_W)[:, None] * inv[None, :]
    cos = np.concatenate([np.cos(ang_r)] * 2 + [np.cos(ang_c)] * 2, axis=1)
    sin = np.concatenate([-np.sin(ang_r), np.sin(ang_r), -np.sin(ang_c), np.sin(ang_c)], axis=1)
    return (np.tile(cos, (1, GROUP_W // HEAD_DIM)).astype(np.float32),
            np.tile(sin, (1, GROUP_W // HEAD_DIM)).astype(np.float32))


@functools.lru_cache(maxsize=None)
def _hyena_consts(seq):
    pos = np.arange(seq, dtype=np.float64)
    bands = (FILTER_EMB - 1) // 2
    ang = (2.0 * math.pi / seq) * pos[:, None] * np.linspace(1e-4, bands - 1, bands)[None, :]
    feats = np.concatenate([(pos / max(seq - 1, 1))[:, None], np.cos(ang), -np.sin(ang)], -1)
    feats = np.pad(feats, ((0, 0), (0, LANES - FILTER_EMB)))
    centre = seq // 2
    dist = np.abs(pos - centre) / max(centre, 1)
    deltas = np.abs(np.linspace(math.log(HY_TARGET) / HY_SLOW_DECAY, math.log(HY_TARGET) / HY_FAST_DECAY,
                                GROUP_W))
    win = np.exp(-dist[:, None] * deltas[None, :])
    k = np.arange(seq, dtype=np.float64)
    fwd = (math.pi / seq) * np.outer(k + 0.5, pos)
    inv = (math.pi / seq) * np.outer(pos + seq // 2, k + 0.5)
    f32 = lambda a: a.astype(np.float32)
    return {"feats": f32(feats), "win": f32(win),
            "cf": f32(np.cos(fwd)), "sf": f32(-np.sin(fwd)),
            "ci": f32(np.cos(inv) / seq), "si": f32(-np.sin(inv) / seq)}


def _layer_weights(l, w1_gate, w1_up, w1_down, w_in, w_out, w2_gate, w2_up, w2_down):
    b = lambda a: a.astype(BF16)
    wi = w_in[l]
    a0, g0, b0 = 0, 4 * GROUP_W, 4 * GROUP_W + N_GATES
    c0 = b0 + 2 * GROUP_W
    d0 = c0 + 2 * GROUP_W
    wgates = wi[:, g0:b0]
    w_proj = (b(wi[:, a0:g0]), b(jnp.pad(wgates, ((0, 0), (0, LANES - N_GATES)))), b(wgates.T),
              b(wi[:, b0:c0]), b(wi[:, c0:d0]), b(wi[:, d0:]))
    return {"ffn1": (b(w1_gate[l]), b(w1_up[l]), b(w1_down[l])),
            "ffn2": (b(w2_gate[l]), b(w2_up[l]), b(w2_down[l])),
            "proj": w_proj, "out": b(w_out[l])}


def kernel(x_prompt, x_sample, state_mlstm_C, state_mlstm_n, state_mlstm_m, cache_gattn_k, cache_gattn_v, cache_swa_k, cache_swa_v, c, c_ctx, w_ada, b_ada, g_ff1, w1_gate, w1_up, w1_down, g_mix, w_in, b_gates, g_mlstm, g_qnorm, g_knorm, sinks, conv_w, conv_b, filt_w1, filt_b1, filt_w2, filt_b2, filt_w3, filt_b3, filt_freq, hyena_bias, w_out, g_ff2, w2_gate, w2_up, w2_down, g_final):
    nb, ns = x_prompt.shape[:2]
    db, dn = x_sample.shape[:2]
    row = lambda a: a.reshape(1, -1)

    cc = jnp.concatenate([c, c_ctx[None, :], jnp.zeros((8 - db - 1, D_MODEL), F32)], axis=0)
    mod_all = _ada(cc, w_ada, b_ada)
    rope = tuple(jnp.asarray(a) for a in _rope_tables(dn))
    streams = [dict(x=x_prompt.reshape(nb * ns, D_MODEL), batch=nb, seq=ns, ctx=True),
               dict(x=x_sample.reshape(db * dn, D_MODEL), batch=db, seq=dn, ctx=False)]
    for st in streams:
        st["dft"] = _dft_tables_bf16(_hyena_consts(st["seq"]))
    new = {name: [] for name in ("C", "n", "m", "gk", "gv", "sk", "sv")}

    for l in range(DEPTH):
        w = _layer_weights(l, w1_gate, w1_up, w1_down, w_in, w_out, w2_gate, w2_up, w2_down)
        bg = b_gates[l]
        bgc = jnp.pad(bg, (0, LANES - N_GATES)).reshape(1, LANES)
        bgr = bg.reshape(N_GATES, 1)
        sink = jnp.pad(sinks[l], (0, LANES - sinks.shape[1])).reshape(1, LANES)
        w1p = jnp.pad(filt_w1[l], ((0, LANES - FILTER_EMB), (0, 0)))
        for st in streams:
            batch, seq = st["batch"], st["seq"]
            if st["ctx"]:
                mod = mod_all[l, db:db + 1].reshape(1, N_MOD, D_MODEL)
            else:
                mod = mod_all[l, :db].reshape(db, N_MOD, D_MODEL)
            x = _ffn(st["x"], mod, row(g_ff1[l]), *w["ffn1"], base=0)
            ua, gc, gr, ub, uc, ud = _inproj(x, mod, row(g_mix[l]), w["proj"])

            if st["ctx"]:
                c0 = jnp.zeros((batch, 2 * H_A, HEAD_DIM, LANES), F32)
                m0 = jnp.zeros((batch, 2 * H_A, LANES), F32)
            else:
                sc = state_mlstm_C[:, l].reshape(batch, 2 * H_A, HEAD_DIM, HEAD_DIM)
                sn = state_mlstm_n[:, l].reshape(batch, 2 * H_A, HEAD_DIM, 1)
                c0 = jnp.concatenate([sc, jnp.broadcast_to(sn, sc.shape)], axis=-1)
                m0 = jnp.broadcast_to(state_mlstm_m[:, l].reshape(batch, 2 * H_A, 1), (batch, 2 * H_A, LANES))
            ya, ct, mt = _mlstm(ua, gc, gr, bgc, bgr, row(g_mlstm[l]), c0, m0, batch, seq)

            consts = _hyena_consts(seq)
            hre, him = _hyena_filter(consts, w1p, row(filt_b1[l]), filt_w2[l], row(filt_b2[l]),
                                     filt_w3[l], row(filt_b3[l]), row(filt_freq[l]))
            yd = _hyena(ud, conv_w[l], row(conv_b[l]), row(hyena_bias[l]), hre, him, st["dft"], batch, seq)

            if st["ctx"]:
                yb, kb_new = _attn(ub, batch, seq, q_tile=seq, gq=row(g_qnorm[l]), gk=row(g_knorm[l]),
                                   emit_k=True)
                (yc,) = _attn(uc, batch, seq, q_tile=seq, sink=sink)
                new["C"].append(ct[..., :HEAD_DIM].reshape(batch, 2, H_A, HEAD_DIM, HEAD_DIM))
                new["n"].append(ct[..., HEAD_DIM].reshape(batch, 2, H_A, HEAD_DIM))
                new["m"].append(mt[..., 0].reshape(batch, 2, H_A))
                kv = lambda a: a.reshape(batch, seq, KV_W // HEAD_DIM, HEAD_DIM)
                new["gk"].append(kv(kb_new))
                new["gv"].append(kv(ub[:, GROUP_W + KV_W:]))
                new["sk"].append(kv(uc[:, GROUP_W:GROUP_W + KV_W]))
                new["sv"].append(kv(uc[:, GROUP_W + KV_W:]))
            else:
                past = cache_gattn_k.shape[2]
                flat = lambda a: a[:, l].reshape(batch, past, KV_W)
                (yb,) = _attn(ub, batch, seq, q_tile=256, gq=row(g_qnorm[l]), gk=row(g_knorm[l]), rope=rope,
                              cache=(flat(cache_gattn_k), flat(cache_gattn_v)))
                (yc,) = _attn(uc, batch, seq, q_tile=WINDOW, rope=rope, sink=sink, window=WINDOW,
                              cache=(flat(cache_swa_k), flat(cache_swa_v)))

            x = _outproj(x, mod, (ya, yb, yc, yd), w["out"])
            st["x"] = _ffn(x, mod, row(g_ff2[l]), *w["ffn2"], base=6,
                           g_final=row(g_final) if l == DEPTH - 1 else None)

    stack = lambda name: jnp.stack(new[name], axis=1)
    return (streams[0]["x"].reshape(nb, ns, D_MODEL), streams[1]["x"].reshape(db, dn, D_MODEL),
            stack("C"), stack("n"), stack("m"), stack("gk"), stack("gv"), stack("sk"), stack("sv"))
```
